```python
import math
import jax, jax.numpy as jnp
from jax import lax
import numpy as np

D_MODEL = 1024
BATCH = 8
SEQ = 4096
DEPTH = 1
DEC_BATCH = 128
DEC_SEQ = 8
PAST_LEN = 16384
PAGE_SIZE = 128

D_CONV = 512
CONV_WIDTH = 3
N_HEADS = 8
N_KV_HEADS = 2
HEAD_DIM = 64
GROUP = N_HEADS // N_KV_HEADS
D_ATTN = N_HEADS * HEAD_DIM
D_KV = N_KV_HEADS * HEAD_DIM
WINDOW = 128
ROPE_THETA = 10000.0
PEER_HEADS = 8
N_KEYS = 128
N_EXPERTS = N_KEYS * N_KEYS
PEER_TOPK = 16
D_QUERY = 256
D_HALF = D_QUERY // 2
PEER_BLOCK = 256
LN_EPS = 1e-5
DEEPNORM_ALPHA = (2.0 * DEPTH) ** 0.25
DEEPNORM_BETA = (8.0 * DEPTH) ** -0.25
IN_SPLITS = [D_CONV, D_CONV, D_CONV, D_ATTN, D_KV, D_KV, D_MODEL, D_MODEL]
D_IN = sum(IN_SPLITS)

kernel_name = "hybrid_shortconv_swa_sink_peer_deepnorm_step"


def _layer_norm(x, g, b):
    xf = x.astype(jnp.float32)
    mu = jnp.mean(xf, axis=-1, keepdims=True)
    var = jnp.mean(jnp.square(xf - mu), axis=-1, keepdims=True)
    return ((xf - mu) * lax.rsqrt(var + LN_EPS) * g + b).astype(x.dtype)


def _rotary(x, pos):
    half = HEAD_DIM // 2
    inv_freq = ROPE_THETA ** (-2.0 * jnp.arange(half, dtype=jnp.float32) / HEAD_DIM)
    ang = pos.astype(jnp.float32)[:, None] * inv_freq[None, :]
    cos = jnp.cos(ang)[:, None, :]
    sin = jnp.sin(ang)[:, None, :]
    xf = x.astype(jnp.float32)
    x1, x2 = xf[..., :half], xf[..., half:]
    return jnp.concatenate([x1 * cos - x2 * sin, x2 * cos + x1 * sin], axis=-1).astype(x.dtype)


def _mixer_projections(x, pos, w_in):
    B, T, _ = x.shape
    z = jnp.einsum('btd,de->bte', x, w_in)
    cuts = np.cumsum(IN_SPLITS)[:-1].tolist()
    b_gate, c_gate, h_in, q, k, v, g_conv, g_attn = jnp.split(z, cuts, axis=-1)
    q = _rotary(q.reshape(B, T, N_HEADS, HEAD_DIM), pos).reshape(B, T, N_KV_HEADS, GROUP, HEAD_DIM)
    k = _rotary(k.reshape(B, T, N_KV_HEADS, HEAD_DIM), pos)
    v = v.reshape(B, T, N_KV_HEADS, HEAD_DIM)
    return b_gate, c_gate * h_in, q, k, v, g_conv, g_attn


def _short_conv(u, past, conv_w):
    T = u.shape[1]
    up = jnp.concatenate([past, u], axis=1)
    y = conv_w[CONV_WIDTH - 1] * up[:, CONV_WIDTH - 1:CONV_WIDTH - 1 + T]
    for j in range(CONV_WIDTH - 1):
        y = y + conv_w[j] * up[:, j:j + T]
    return y, up[:, -(CONV_WIDTH - 1):]


def _sink_attention(q, k, v, q_pos, k_pos, sinks):
    s = jnp.einsum('...qkgd,...skd->...kgqs', q, k,
                   preferred_element_type=jnp.float32) * (HEAD_DIM ** -0.5)
    d = q_pos[..., :, None] - k_pos[..., None, :]
    mask = (d >= 0) & (d < WINDOW) & (k_pos[..., None, :] >= 0)
    s = jnp.where(mask[..., None, None, :, :], s, -jnp.inf)
    sink = sinks.astype(jnp.float32).reshape(N_KV_HEADS, GROUP)[:, :, None, None]
    m = jnp.maximum(jnp.max(s, axis=-1, keepdims=True), sink)
    p = jnp.exp(s - m)
    denom = jnp.sum(p, axis=-1, keepdims=True) + jnp.exp(sink - m)
    w = (p / denom).astype(v.dtype)
    return jnp.einsum('...kgqs,...skd->...qkgd', w, v)


def _banded_attention(q, k, v, sinks):
    B, S = q.shape[:2]
    nb = S // WINDOW
    qb = q.reshape(B, nb, WINDOW, N_KV_HEADS, GROUP, HEAD_DIM)

    def band(t):
        prev = jnp.pad(t, ((0, 0), (WINDOW, 0), (0, 0), (0, 0)))[:, :S]
        prev = prev.reshape(B, nb, WINDOW, N_KV_HEADS, HEAD_DIM)
        return jnp.concatenate([prev, t.reshape(B, nb, WINDOW, N_KV_HEADS, HEAD_DIM)], axis=2)

    q_pos = jnp.arange(S, dtype=jnp.int32).reshape(nb, WINDOW)
    k_pos = jnp.concatenate([q_pos - WINDOW, q_pos], axis=-1)
    o = _sink_attention(qb, band(k), band(v), q_pos, k_pos, sinks)
    return o.reshape(B, S, D_ATTN)


def _peer(h, w_pq, sub_keys, expert_u, expert_v):
    B, T, D = h.shape
    n = B * T
    flat = jnp.pad(h.reshape(n, D), ((0, (-n) % PEER_BLOCK), (0, 0)))
    blocks = flat.reshape(-1, PEER_BLOCK, D)

    def block_fn(xb):
        qv = jnp.einsum('nd,de->ne', xb, w_pq).reshape(PEER_BLOCK, PEER_HEADS, 2, D_HALF)
        s = jnp.einsum('nhpc,hpkc->nhpk', qv, sub_keys,
                       preferred_element_type=jnp.float32)
        top_s, top_i = lax.top_k(s, PEER_TOPK)
        cand_s = top_s[:, :, 0, :, None] + top_s[:, :, 1, None, :]
        cand_i = top_i[:, :, 0, :, None] * N_KEYS + top_i[:, :, 1, None, :]
        best_s, sel = lax.top_k(cand_s.reshape(PEER_BLOCK, PEER_HEADS, -1), PEER_TOPK)
        idx = jnp.take_along_axis(cand_i.reshape(PEER_BLOCK, PEER_HEADS, -1), sel, axis=-1)
        gate = jax.nn.softmax(best_s, axis=-1)
        pre = jnp.einsum('nd,nhkd->nhk', xb, expert_u[idx], preferred_element_type=jnp.float32)
        coef = (gate * jax.nn.gelu(pre, approximate=False)).astype(xb.dtype)
        return jnp.einsum('nhk,nhkd->nd', coef, expert_v[idx])

    out = lax.map(block_fn, blocks)
    return out.reshape(-1, D)[:n].reshape(B, T, D)


def _merge_and_channel_mix(x, b_gate, conv_y, attn_y, g_conv, g_attn,
                           w_conv_out, w_attn_out, w_o, ln1_g, ln1_b,
                           w_pq, sub_keys, expert_u, expert_v, ln2_g, ln2_b):
    conv_br = jnp.einsum('btc,cd->btd', b_gate * conv_y, w_conv_out)
    attn_br = jnp.einsum('bte,ed->btd', attn_y, w_attn_out)
    merged = jax.nn.sigmoid(g_conv) * conv_br + jax.nn.sigmoid(g_attn) * attn_br
    mix = jnp.einsum('btd,de->bte', merged, w_o)
    h = _layer_norm(DEEPNORM_ALPHA * x + mix, ln1_g, ln1_b)
    return _layer_norm(DEEPNORM_ALPHA * h + _peer(h, w_pq, sub_keys, expert_u, expert_v), ln2_g, ln2_b)


def setup_inputs(seed: int = 0) -> dict:
    key = jax.random.key(seed)
    ks = jax.random.split(key, 20)
    f32 = jnp.float32
    w_buf = min(WINDOW, PAST_LEN)
    nrm = lambda k, shape: jax.random.normal(k, shape, f32)
    col_scale = jnp.concatenate([
        jnp.ones((3 * D_CONV + D_ATTN + D_KV,), f32),
        jnp.full((D_KV,), DEEPNORM_BETA, f32),
        jnp.ones((2 * D_MODEL,), f32)])
    return {
        "x_prompt": nrm(ks[0], (BATCH, SEQ, D_MODEL)),
        "x_sample": nrm(ks[1], (DEC_BATCH, DEC_SEQ, D_MODEL)),
        "state_conv": nrm(ks[2], (DEC_BATCH, CONV_WIDTH - 1, D_CONV)),
        "cache_k": nrm(ks[3], (DEC_BATCH, w_buf, N_KV_HEADS, HEAD_DIM)),
        "cache_v": nrm(ks[4], (DEC_BATCH, w_buf, N_KV_HEADS, HEAD_DIM)) * DEEPNORM_BETA,
        "w_in": nrm(ks[5], (D_MODEL, D_IN)) * D_MODEL ** -0.5 * col_scale,
        "conv_w": nrm(ks[6], (CONV_WIDTH, D_CONV)) * CONV_WIDTH ** -0.5,
        "attn_sinks": nrm(ks[7], (N_HEADS,)) * 0.5,
        "w_conv_out": nrm(ks[8], (D_CONV, D_MODEL)) * D_CONV ** -0.5 * DEEPNORM_BETA,
        "w_attn_out": nrm(ks[9], (D_ATTN, D_MODEL)) * D_ATTN ** -0.5 * DEEPNORM_BETA,
        "w_o": nrm(ks[10], (D_MODEL, D_MODEL)) * D_MODEL ** -0.5 * DEEPNORM_BETA,
        "ln1_g": 1.0 + 0.02 * nrm(ks[11], (D_MODEL,)),
        "ln1_b": 0.02 * nrm(ks[12], (D_MODEL,)),
        "w_pq": nrm(ks[13], (D_MODEL, PEER_HEADS * D_QUERY)) * D_MODEL ** -0.5,
        "sub_keys": nrm(ks[14], (PEER_HEADS, 2, N_KEYS, D_HALF)) * D_HALF ** -0.5,
        "expert_u": nrm(ks[15], (N_EXPERTS, D_MODEL)) * D_MODEL ** -0.5,
        "expert_v": nrm(ks[16], (N_EXPERTS, D_MODEL)) * DEEPNORM_BETA * PEER_HEADS ** -0.5,
        "ln2_g": 1.0 + 0.02 * nrm(ks[17], (D_MODEL,)),
        "ln2_b": 0.02 * nrm(ks[18], (D_MODEL,)),
    }


def reference(x_prompt, x_sample, state_conv, cache_k, cache_v, w_in, conv_w, attn_sinks,
              w_conv_out, w_attn_out, w_o, ln1_g, ln1_b, w_pq, sub_keys, expert_u, expert_v,
              ln2_g, ln2_b):
    B, S, _ = x_prompt.shape
    pos_p = jnp.arange(S, dtype=jnp.int32)
    b_p, u_p, q_p, k_p, v_p, gc_p, ga_p = _mixer_projections(x_prompt, pos_p, w_in)
    conv_past_p = jnp.zeros((B, CONV_WIDTH - 1, D_CONV), x_prompt.dtype)
    conv_p, new_conv_p = _short_conv(u_p, conv_past_p, conv_w)
    attn_p = _banded_attention(q_p, k_p, v_p, attn_sinks)
    y_prompt = _merge_and_channel_mix(x_prompt, b_p, conv_p, attn_p, gc_p, ga_p,
                                      w_conv_out, w_attn_out, w_o, ln1_g, ln1_b,
                                      w_pq, sub_keys, expert_u, expert_v, ln2_g, ln2_b)
    w_p = min(WINDOW, S)
    new_k_p = k_p[:, S - w_p:]
    new_v_p = v_p[:, S - w_p:]

    Bd, T, _ = x_sample.shape
    w_s = cache_k.shape[1]
    pos_s = PAST_LEN + jnp.arange(T, dtype=jnp.int32)
    b_s, u_s, q_s, k_s, v_s, gc_s, ga_s = _mixer_projections(x_sample, pos_s, w_in)
    conv_s, new_conv_s = _short_conv(u_s, state_conv, conv_w)
    k_all = jnp.concatenate([cache_k, k_s], axis=1)
    v_all = jnp.concatenate([cache_v, v_s], axis=1)
    k_pos = jnp.concatenate([PAST_LEN - w_s + jnp.arange(w_s, dtype=jnp.int32), pos_s])
    attn_s = _sink_attention(q_s, k_all, v_all, pos_s, k_pos, attn_sinks).reshape(Bd, T, D_ATTN)
    y_sample = _merge_and_channel_mix(x_sample, b_s, conv_s, attn_s, gc_s, ga_s,
                                      w_conv_out, w_attn_out, w_o, ln1_g, ln1_b,
                                      w_pq, sub_keys, expert_u, expert_v, ln2_g, ln2_b)
    new_k_s = k_all[:, -w_s:]
    new_v_s = v_all[:, -w_s:]

    return (y_prompt, y_sample, new_conv_p, new_k_p, new_v_p, new_conv_s, new_k_s, new_v_s)
```

```python
import functools

import jax
import jax.numpy as jnp
from jax import lax
from jax.experimental import pallas as pl
from jax.experimental.pallas import tpu as pltpu

F32 = jnp.float32
BF16 = jnp.bfloat16

D_MODEL = 1024
DEPTH = 1
PAST_LEN = 16384
D_CONV = 512
CONV_WIDTH = 3
N_HEADS = 8
N_KV_HEADS = 2
HEAD_DIM = 64
D_ATTN = N_HEADS * HEAD_DIM
D_KV = N_KV_HEADS * HEAD_DIM
WINDOW = 128
ROPE_THETA = 10000.0
PEER_HEADS = 8
N_KEYS = 128
N_EXPERTS = N_KEYS * N_KEYS
PEER_TOPK = 16
D_QUERY = 256
D_HALF = D_QUERY // 2
LN_EPS = 1e-5
DEEPNORM_ALPHA = (2.0 * DEPTH) ** 0.25

_OFF_B = 0
_OFF_C = _OFF_B + D_CONV
_OFF_H = _OFF_C + D_CONV
_OFF_Q = _OFF_H + D_CONV
_OFF_K = _OFF_Q + D_ATTN
_OFF_V = _OFF_K + D_KV
_OFF_GC = _OFF_V + D_KV
_OFF_GA = _OFF_GC + D_MODEL
D_IN = _OFF_GA + D_MODEL

V7X_LANES = 128
V7X_SUBLANES = 8
V7X_BF16_ROWS = 16
V7X_VMEM_LIMIT_BYTES = 56 * 1024 * 1024

MIX_TILE = 512
SAMPLE_SEQS = 32
PREP_TILE = 256
PEER_TILE = 1024
PEER_BLOCK = 256
PEER_CHUNK = 1024


def _dot(a, b):
    return jnp.dot(a, b, preferred_element_type=F32)


def _dot_nt(a, b):
    return lax.dot_general(a, b, (((1,), (1,)), ((), ())), preferred_element_type=F32)


def _layer_norm(x, g, b):
    mu = jnp.mean(x, axis=-1, keepdims=True)
    xc = x - mu
    var = jnp.mean(xc * xc, axis=-1, keepdims=True)
    return xc * lax.rsqrt(var + LN_EPS) * g + b


def _sigmoid(x):
    return 1.0 / (1.0 + jnp.exp(-x))


def _rope(t, cos, sin_signed):
    lane = lax.broadcasted_iota(jnp.int32, (t.shape[0], V7X_LANES), 1)
    first_half = (lane % HEAD_DIM) < (HEAD_DIM // 2)
    outs = []
    for c in range(t.shape[1] // V7X_LANES):
        tc = t[:, c * V7X_LANES:(c + 1) * V7X_LANES]
        partner = jnp.where(first_half,
                            pltpu.roll(tc, V7X_LANES - HEAD_DIM // 2, 1),
                            pltpu.roll(tc, HEAD_DIM // 2, 1))
        outs.append(tc * cos + partner * sin_signed)
    return outs[0] if len(outs) == 1 else jnp.concatenate(outs, axis=1)


def _kv_variants(t):
    lane = lax.broadcasted_iota(jnp.int32, t.shape, 1)
    lo = lane < HEAD_DIM
    sw = pltpu.roll(t, HEAD_DIM, 1)
    zero = jnp.zeros_like(t)
    lo0 = jnp.where(lo, t, zero).astype(BF16)
    hi0 = jnp.where(lo, zero, sw).astype(BF16)
    lo1 = jnp.where(lo, sw, zero).astype(BF16)
    hi1 = jnp.where(lo, zero, t).astype(BF16)
    return ((lo0, hi0), (lo1, hi1))


def _sink_softmax(s, valid, sink_col):
    s = jnp.where(valid, s * (HEAD_DIM ** -0.5), -jnp.inf)
    m = jnp.maximum(jnp.max(s, axis=1, keepdims=True), sink_col)
    p = jnp.exp(s - m)
    den = jnp.sum(p, axis=1, keepdims=True) + jnp.exp(sink_col - m)
    return (p / den).astype(BF16)


def _attend(q_pairs, kvar, vvar, valid, sinks_ref, kvh, rows):
    q2 = jnp.concatenate(q_pairs, axis=0)
    rix = lax.broadcasted_iota(jnp.int32, (2 * rows, 1), 0)
    base = kvh * 4
    sink_a = jnp.where(rix < rows, sinks_ref[base + 0], sinks_ref[base + 2])
    sink_b = jnp.where(rix < rows, sinks_ref[base + 1], sinks_ref[base + 3])
    w_a = _sink_softmax(_dot_nt(q2, kvar[0]), valid, sink_a)
    w_b = _sink_softmax(_dot_nt(q2, kvar[1]), valid, sink_b)
    o = _dot(jnp.concatenate([w_a, w_b], axis=1), jnp.concatenate([vvar[0], vvar[1]], axis=0))
    return o[:rows], o[rows:]


def _merge(x, b_gate, conv_y, attn_y, g_conv, g_attn, wco_ref, wao_ref, wo_ref, g_ref, b_ref):
    conv_br = _dot((b_gate * conv_y).astype(BF16), wco_ref[...])
    attn_br = _dot(attn_y.astype(BF16), wao_ref[...])
    merged = _sigmoid(g_conv) * conv_br + _sigmoid(g_attn) * attn_br
    mix = _dot(merged.astype(BF16), wo_ref[...])
    return _layer_norm(DEEPNORM_ALPHA * x + mix, g_ref[...], b_ref[...])


def _prompt_mixer_kernel(sinks_ref, x_ref, cos_ref, sin_ref, w_in_ref, convw_ref, wco_ref, wao_ref,
                         wo_ref, g_ref, b_ref, h_ref, kc_ref, vc_ref, cs_ref, kcar, vcar, ucar):
    j = pl.program_id(1)
    nj = pl.num_programs(1)
    tile = x_ref.shape[1]
    nblk = tile // WINDOW

    @pl.when(j == 0)
    def _():
        kcar[...] = jnp.zeros_like(kcar)
        vcar[...] = jnp.zeros_like(vcar)
        ucar[...] = jnp.zeros_like(ucar)

    x = x_ref[0]
    xb = x.astype(BF16)
    cos = cos_ref[...]
    sin = sin_ref[...]

    def proj(lo, hi):
        return _dot(xb, w_in_ref[:, lo:hi])

    u = proj(_OFF_C, _OFF_H) * proj(_OFF_H, _OFF_Q)
    row = lax.broadcasted_iota(jnp.int32, u.shape, 0)
    uc = ucar[...]
    prev1 = jnp.where(row == 0, uc[7:8], pltpu.roll(u, 1, 0))
    prev2 = jnp.where(row == 0, uc[6:7], jnp.where(row == 1, uc[7:8], pltpu.roll(u, 2, 0)))
    cw = convw_ref[...]
    conv_y = cw[2:3] * u
    conv_y = conv_y + cw[0:1] * prev2
    conv_y = conv_y + cw[1:2] * prev1

    q = _rope(proj(_OFF_Q, _OFF_K), cos, sin).astype(BF16)
    k = _rope(proj(_OFF_K, _OFF_V), cos, sin)
    v = proj(_OFF_V, _OFF_GC)
    kvar = _kv_variants(jnp.concatenate([kcar[...], k], axis=0))
    vvar = _kv_variants(jnp.concatenate([vcar[...], v], axis=0))
    r = lax.broadcasted_iota(jnp.int32, (2 * WINDOW, 2 * WINDOW), 0) % WINDOW
    c = lax.broadcasted_iota(jnp.int32, (2 * WINDOW, 2 * WINDOW), 1)
    in_window = (c > r) & (c <= r + WINDOW)
    row_blocks = []
    for jb in range(nblk):
        r0 = jb * WINDOW
        first_key = jnp.where(j * nblk + jb > 0, 0, WINDOW)
        valid = in_window & (c >= first_key)
        cols = []
        for kvh in range(N_KV_HEADS):
            qp = [q[r0:r0 + WINDOW, (2 * kvh + i) * V7X_LANES:(2 * kvh + i + 1) * V7X_LANES] for i in range(2)]
            kv_k = tuple(a[r0:r0 + 2 * WINDOW] for a in kvar[kvh])
            kv_v = tuple(a[r0:r0 + 2 * WINDOW] for a in vvar[kvh])
            cols.extend(_attend(qp, kv_k, kv_v, valid, sinks_ref, kvh, WINDOW))
        row_blocks.append(jnp.concatenate(cols, axis=1))
    attn_y = jnp.concatenate(row_blocks, axis=0)

    h_ref[0] = _merge(x, proj(_OFF_B, _OFF_C), conv_y, attn_y, proj(_OFF_GC, _OFF_GA),
                      proj(_OFF_GA, D_IN), wco_ref, wao_ref, wo_ref, g_ref, b_ref)

    kcar[...] = k[tile - WINDOW:]
    vcar[...] = v[tile - WINDOW:]
    ucar[...] = u[tile - V7X_SUBLANES:]

    @pl.when(j == nj - 1)
    def _():
        kc_ref[0] = k[tile - WINDOW:]
        vc_ref[0] = v[tile - WINDOW:]
        cs_ref[0] = u[tile - (CONV_WIDTH - 1):]


def _const_spec(shape):
    nd = len(shape)
    return pl.BlockSpec(shape, lambda *_: (0,) * nd, pipeline_mode=pl.Buffered(1))


def _rope_tables(pos):
    half = HEAD_DIM // 2
    inv_freq = ROPE_THETA ** (-2.0 * jnp.arange(half, dtype=F32) / HEAD_DIM)
    ang = pos.astype(F32)[:, None] * inv_freq[None, :]
    cos = jnp.cos(ang)
    sin = jnp.sin(ang)
    reps = V7X_LANES // HEAD_DIM
    cos_t = jnp.tile(jnp.concatenate([cos, cos], axis=1), (1, reps))
    sin_t = jnp.tile(jnp.concatenate([-sin, sin], axis=1), (1, reps))
    return cos_t, sin_t


def _prompt_mixer(x, sinks, w_in, conv_w, wco, wao, wo, g, b):
    bsz, seq, _ = x.shape
    tile = MIX_TILE
    assert seq % tile == 0 and tile % WINDOW == 0 and seq >= WINDOW
    cos_t, sin_t = _rope_tables(jnp.arange(seq, dtype=jnp.int32))
    grid_spec = pltpu.PrefetchScalarGridSpec(
        num_scalar_prefetch=1,
        grid=(bsz, seq // tile),
        in_specs=[
            pl.BlockSpec((1, tile, D_MODEL), lambda bi, j, s: (bi, j, 0)),
            pl.BlockSpec((tile, V7X_LANES), lambda bi, j, s: (j, 0)),
            pl.BlockSpec((tile, V7X_LANES), lambda bi, j, s: (j, 0)),
            _const_spec((D_MODEL, D_IN)),
            _const_spec((CONV_WIDTH, D_CONV)),
            _const_spec((D_CONV, D_MODEL)),
            _const_spec((D_ATTN, D_MODEL)),
            _const_spec((D_MODEL, D_MODEL)),
            _const_spec((1, D_MODEL)),
            _const_spec((1, D_MODEL)),
        ],
        out_specs=[
            pl.BlockSpec((1, tile, D_MODEL), lambda bi, j, s: (bi, j, 0)),
            pl.BlockSpec((1, WINDOW, D_KV), lambda bi, j, s: (bi, 0, 0)),
            pl.BlockSpec((1, WINDOW, D_KV), lambda bi, j, s: (bi, 0, 0)),
            pl.BlockSpec((1, CONV_WIDTH - 1, D_CONV), lambda bi, j, s: (bi, 0, 0)),
        ],
        scratch_shapes=[
            pltpu.VMEM((WINDOW, D_KV), F32),
            pltpu.VMEM((WINDOW, D_KV), F32),
            pltpu.VMEM((V7X_SUBLANES, D_CONV), F32),
        ],
    )
    return pl.pallas_call(
        _prompt_mixer_kernel,
        grid_spec=grid_spec,
        out_shape=[
            jax.ShapeDtypeStruct((bsz, seq, D_MODEL), F32),
            jax.ShapeDtypeStruct((bsz, WINDOW, D_KV), F32),
            jax.ShapeDtypeStruct((bsz, WINDOW, D_KV), F32),
            jax.ShapeDtypeStruct((bsz, CONV_WIDTH - 1, D_CONV), F32),
        ],
        compiler_params=pltpu.CompilerParams(
            dimension_semantics=("arbitrary", "arbitrary"),
            vmem_limit_bytes=V7X_VMEM_LIMIT_BYTES),
        name="prompt_mixer",
    )(sinks, x, cos_t, sin_t, w_in, conv_w, wco, wao, wo, g, b)


def _sample_mixer_kernel(sinks_ref, x_ref, cos_ref, sin_ref, st_ref, ck_ref, cv_ref, w_in_ref, convw_ref,
                         wco_ref, wao_ref, wo_ref, g_ref, b_ref,
                         h_ref, cs_ref, nk_ref, nv_ref,
                         q_scr, k_scr, v_scr, u_scr, cy_scr, ay_scr):
    nseq = ck_ref.shape[0]
    wbuf = ck_ref.shape[1]
    t_new = x_ref.shape[0] // nseq
    x = x_ref[...]
    xb = x.astype(BF16)
    cos = cos_ref[...]
    sin = sin_ref[...]

    def proj(lo, hi):
        return _dot(xb, w_in_ref[:, lo:hi])

    u_scr[...] = proj(_OFF_C, _OFF_H) * proj(_OFF_H, _OFF_Q)
    q_scr[...] = _rope(proj(_OFF_Q, _OFF_K), cos, sin)
    k_scr[...] = _rope(proj(_OFF_K, _OFF_V), cos, sin)
    v_scr[...] = proj(_OFF_V, _OFF_GC)
    cw = convw_ref[...]

    span = 2 * WINDOW
    r = lax.broadcasted_iota(jnp.int32, (2 * t_new, span), 0) % t_new
    c = lax.broadcasted_iota(jnp.int32, (2 * t_new, span), 1)
    valid = ((c < wbuf) & (c > r + (wbuf - WINDOW))) | ((c >= wbuf) & (c - wbuf <= r))
    pad = jnp.zeros((span - wbuf - t_new, D_KV), F32)
    row = lax.broadcasted_iota(jnp.int32, (t_new, D_CONV), 0)

    def per_seq(bi, carry):
        rows = pl.ds(pl.multiple_of(bi * t_new, t_new), t_new)
        u = u_scr[rows, :]
        st = st_ref[bi]
        prev1 = jnp.where(row == 0, st[1:2], pltpu.roll(u, 1, 0))
        prev2 = jnp.where(row == 0, st[0:1], jnp.where(row == 1, st[1:2], pltpu.roll(u, 2, 0)))
        cy = cw[2:3] * u
        cy = cy + cw[0:1] * prev2
        cy = cy + cw[1:2] * prev1
        cy_scr[rows, :] = cy
        cs_ref[bi] = u[t_new - (CONV_WIDTH - 1):]
        q = q_scr[rows, :].astype(BF16)
        kn = k_scr[rows, :]
        vn = v_scr[rows, :]
        ck = ck_ref[bi]
        cv = cv_ref[bi]
        kvar = _kv_variants(jnp.concatenate([ck, kn, pad], axis=0))
        vvar = _kv_variants(jnp.concatenate([cv, vn, pad], axis=0))
        cols = []
        for kvh in range(N_KV_HEADS):
            qp = [q[:, (2 * kvh + i) * V7X_LANES:(2 * kvh + i + 1) * V7X_LANES] for i in range(2)]
            cols.extend(_attend(qp, kvar[kvh], vvar[kvh], valid, sinks_ref, kvh, t_new))
        ay_scr[rows, :] = jnp.concatenate(cols, axis=1)
        nk_ref[bi] = jnp.concatenate([ck[t_new:], kn], axis=0)
        nv_ref[bi] = jnp.concatenate([cv[t_new:], vn], axis=0)
        return carry

    lax.fori_loop(0, nseq, per_seq, 0)

    h_ref[...] = _merge(x, proj(_OFF_B, _OFF_C), cy_scr[...], ay_scr[...], proj(_OFF_GC, _OFF_GA),
                        proj(_OFF_GA, D_IN), wco_ref, wao_ref, wo_ref, g_ref, b_ref)


def _sample_mixer(x, state_conv, cache_k, cache_v, sinks, w_in, conv_w, wco, wao, wo, g, b):
    nb, t_new, _ = x.shape
    wbuf = cache_k.shape[1]
    nseq = SAMPLE_SEQS
    assert nb % nseq == 0 and t_new == V7X_SUBLANES and wbuf == WINDOW
    rows = nseq * t_new
    cos_t, sin_t = _rope_tables(PAST_LEN + jnp.arange(t_new, dtype=jnp.int32))
    cos_t = jnp.tile(cos_t, (nseq, 1))
    sin_t = jnp.tile(sin_t, (nseq, 1))
    xf = x.reshape(nb * t_new, D_MODEL)
    ck = cache_k.reshape(nb, wbuf, D_KV)
    cv = cache_v.reshape(nb, wbuf, D_KV)
    grid_spec = pltpu.PrefetchScalarGridSpec(
        num_scalar_prefetch=1,
        grid=(nb // nseq,),
        in_specs=[
            pl.BlockSpec((rows, D_MODEL), lambda i, s: (i, 0)),
            _const_spec((rows, V7X_LANES)),
            _const_spec((rows, V7X_LANES)),
            pl.BlockSpec((nseq, CONV_WIDTH - 1, D_CONV), lambda i, s: (i, 0, 0)),
            pl.BlockSpec((nseq, wbuf, D_KV), lambda i, s: (i, 0, 0)),
            pl.BlockSpec((nseq, wbuf, D_KV), lambda i, s: (i, 0, 0)),
            _const_spec((D_MODEL, D_IN)),
            _const_spec((CONV_WIDTH, D_CONV)),
            _const_spec((D_CONV, D_MODEL)),
            _const_spec((D_ATTN, D_MODEL)),
            _const_spec((D_MODEL, D_MODEL)),
            _const_spec((1, D_MODEL)),
            _const_spec((1, D_MODEL)),
        ],
        out_specs=[
            pl.BlockSpec((rows, D_MODEL), lambda i, s: (i, 0)),
            pl.BlockSpec((nseq, CONV_WIDTH - 1, D_CONV), lambda i, s: (i, 0, 0)),
            pl.BlockSpec((nseq, wbuf, D_KV), lambda i, s: (i, 0, 0)),
            pl.BlockSpec((nseq, wbuf, D_KV), lambda i, s: (i, 0, 0)),
        ],
        scratch_shapes=[
            pltpu.VMEM((rows, D_ATTN), F32),
            pltpu.VMEM((rows, D_KV), F32),
            pltpu.VMEM((rows, D_KV), F32),
            pltpu.VMEM((rows, D_CONV), F32),
            pltpu.VMEM((rows, D_CONV), F32),
            pltpu.VMEM((rows, D_ATTN), F32),
        ],
    )
    h, cs, nk, nv = pl.pallas_call(
        _sample_mixer_kernel,
        grid_spec=grid_spec,
        out_shape=[
            jax.ShapeDtypeStruct((nb * t_new, D_MODEL), F32),
            jax.ShapeDtypeStruct((nb, CONV_WIDTH - 1, D_CONV), F32),
            jax.ShapeDtypeStruct((nb, wbuf, D_KV), F32),
            jax.ShapeDtypeStruct((nb, wbuf, D_KV), F32),
        ],
        compiler_params=pltpu.CompilerParams(
            dimension_semantics=("arbitrary",),
            vmem_limit_bytes=V7X_VMEM_LIMIT_BYTES),
        name="sample_mixer",
    )(sinks, xf, cos_t, sin_t, state_conv, ck, cv, w_in, conv_w, wco, wao, wo, g, b)
    return h, cs, nk, nv


def _top16_desc(x):
    rank = jnp.full(x.shape, float(PEER_TOPK), F32)
    vals = []
    for rnd in range(PEER_TOPK):
        m = jnp.max(x, axis=0, keepdims=True)
        hit = x == m
        rank = jnp.where(hit, float(rnd), rank)
        x = jnp.where(hit, -jnp.inf, x)
        vals.append(m)
    return jnp.concatenate(vals, axis=0), rank


def _dup_bf16_words(x):
    bits = pltpu.bitcast(x.astype(BF16).astype(F32), jnp.uint32)
    return bits | (bits >> 16)


def _peer_prep_kernel(h_ref, wpq_ref, sk_ref, rank2_ref, e2_ref, lrow_ref, e1row_ref, s_scr):
    tile = h_ref.shape[0]
    qv = _dot(h_ref[...].astype(BF16), wpq_ref[...]).astype(BF16)
    for hp in range(2 * PEER_HEADS):
        s_scr[hp] = _dot_nt(sk_ref[hp], qv[:, hp * D_HALF:(hp + 1) * D_HALF])

    def per_head(hd, carry):
        s1 = s_scr[2 * hd]
        s2 = s_scr[2 * hd + 1]
        a_vals, _ = _top16_desc(s1)
        b_vals, rank2 = _top16_desc(s2)
        cand = jnp.concatenate([a_vals[a:a + 1] + b_vals for a in range(PEER_TOPK)], axis=0)
        x = cand
        for _ in range(PEER_TOPK):
            tau = jnp.max(x, axis=0, keepdims=True)
            x = jnp.where(x == tau, -jnp.inf, x)
        sel = cand >= tau
        top = a_vals[0:1] + b_vals[0:1]
        z = jnp.sum(jnp.where(sel, jnp.exp(cand - top), 0.0), axis=0, keepdims=True)
        lrow = jnp.zeros(s1.shape, F32)
        for bidx in range(PEER_TOPK):
            lrow = lrow + jnp.where(s1 + b_vals[bidx:bidx + 1] >= tau, 1.0, 0.0)
        e1 = jnp.exp(s1 - a_vals[0:1])
        e2 = jnp.exp(s2 - b_vals[0:1]) / z
        r2 = rank2.astype(BF16)
        e2b = e2.astype(BF16)
        for gidx in range(N_KEYS // V7X_BF16_ROWS):
            rs = slice(gidx * V7X_BF16_ROWS, (gidx + 1) * V7X_BF16_ROWS)
            rank2_ref[hd, gidx] = r2[rs]
            e2_ref[hd, gidx] = e2b[rs]
        lrow_ref[hd] = _dup_bf16_words(lrow)
        e1row_ref[hd] = _dup_bf16_words(e1)
        return carry

    lax.fori_loop(0, PEER_HEADS, per_head, 0)


def _peer_prep(h, wpq, sk):
    n = h.shape[0]
    tile = PREP_TILE
    assert n % tile == 0
    ngrp = N_KEYS // V7X_BF16_ROWS
    return pl.pallas_call(
        _peer_prep_kernel,
        grid=(n // tile,),
        in_specs=[
            pl.BlockSpec((tile, D_MODEL), lambda i: (i, 0)),
            _const_spec((D_MODEL, PEER_HEADS * D_QUERY)),
            _const_spec((2 * PEER_HEADS, N_KEYS, D_HALF)),
        ],
        out_specs=[
            pl.BlockSpec((PEER_HEADS, ngrp, V7X_BF16_ROWS, tile), lambda i: (0, 0, 0, i)),
            pl.BlockSpec((PEER_HEADS, ngrp, V7X_BF16_ROWS, tile), lambda i: (0, 0, 0, i)),
            pl.BlockSpec((PEER_HEADS, N_KEYS, tile), lambda i: (0, 0, i)),
            pl.BlockSpec((PEER_HEADS, N_KEYS, tile), lambda i: (0, 0, i)),
        ],
        out_shape=[
            jax.ShapeDtypeStruct((PEER_HEADS, ngrp, V7X_BF16_ROWS, n), BF16),
            jax.ShapeDtypeStruct((PEER_HEADS, ngrp, V7X_BF16_ROWS, n), BF16),
            jax.ShapeDtypeStruct((PEER_HEADS, N_KEYS, n), jnp.uint32),
            jax.ShapeDtypeStruct((PEER_HEADS, N_KEYS, n), jnp.uint32),
        ],
        scratch_shapes=[pltpu.VMEM((2 * PEER_HEADS, N_KEYS, tile), F32)],
        compiler_params=pltpu.CompilerParams(
            dimension_semantics=("arbitrary",),
            vmem_limit_bytes=V7X_VMEM_LIMIT_BYTES),
        name="peer_prep",
    )(h, wpq, sk)


def _gelu(x):
    return 0.5 * x * (1.0 + lax.erf(x * (2.0 ** -0.5)))


def _peer_main_kernel(h_ref, u_ref, vt_ref, rank2_ref, e2_ref, lrow_ref, e1row_ref, g_ref, b_ref,
                      y_ref, ht_scr, acc_scr):
    c = pl.program_id(1)
    nc = pl.num_programs(1)
    tile = h_ref.shape[0]
    chunk = u_ref.shape[0]
    blk = PEER_BLOCK

    @pl.when(c == 0)
    def _():
        ht_scr[...] = h_ref[...].T.astype(BF16)
        acc_scr[...] = jnp.zeros_like(acc_scr)

    for tb in range(tile // blk):
        cols = slice(tb * blk, (tb + 1) * blk)
        pre_t = _dot(u_ref[...], ht_scr[:, cols])
        coef_rows = []
        for il in range(chunk // N_KEYS):
            act = _gelu(pre_t[il * N_KEYS:(il + 1) * N_KEYS]).astype(BF16)
            gate = None
            for hd in range(PEER_HEADS):
                lw = jnp.broadcast_to(lrow_ref[hd, il:il + 1, cols], (V7X_SUBLANES, blk))
                ew = jnp.broadcast_to(e1row_ref[hd, il:il + 1, cols], (V7X_SUBLANES, blk))
                l_b = pltpu.bitcast(lw, BF16)
                e1_b = pltpu.bitcast(ew, BF16)
                hit = rank2_ref[hd, :, :, cols] < l_b[None]
                term = jnp.where(hit, e2_ref[hd, :, :, cols], jnp.zeros((), BF16)) * e1_b[None]
                gate = term if gate is None else gate + term
            coef_rows.append(gate.reshape(N_KEYS, blk) * act)
        coef_t = jnp.concatenate(coef_rows, axis=0)
        acc_scr[:, cols] += _dot(vt_ref[...], coef_t)

    @pl.when(c == nc - 1)
    def _():
        for tb in range(tile // blk):
            rows = slice(tb * blk, (tb + 1) * blk)
            peer = acc_scr[:, rows].T
            y_ref[rows, :] = _layer_norm(DEEPNORM_ALPHA * h_ref[rows, :] + peer, g_ref[...], b_ref[...])


def _peer_main(h, u_bf, vt_bf, rank2, e2, lrow, e1row, g, b):
    n = h.shape[0]
    tile = min(PEER_TILE, n)
    chunk = PEER_CHUNK
    assert n % tile == 0 and tile % PEER_BLOCK == 0 and N_EXPERTS % chunk == 0 and chunk % N_KEYS == 0
    ngrp = N_KEYS // V7X_BF16_ROWS
    i1_per_chunk = chunk // N_KEYS
    return pl.pallas_call(
        _peer_main_kernel,
        grid=(n // tile, N_EXPERTS // chunk),
        in_specs=[
            pl.BlockSpec((tile, D_MODEL), lambda s, c: (s, 0)),
            pl.BlockSpec((chunk, D_MODEL), lambda s, c: (c, 0)),
            pl.BlockSpec((D_MODEL, chunk), lambda s, c: (0, c)),
            pl.BlockSpec((PEER_HEADS, ngrp, V7X_BF16_ROWS, tile), lambda s, c: (0, 0, 0, s)),
            pl.BlockSpec((PEER_HEADS, ngrp, V7X_BF16_ROWS, tile), lambda s, c: (0, 0, 0, s)),
            pl.BlockSpec((PEER_HEADS, i1_per_chunk, tile), lambda s, c: (0, c, s)),
            pl.BlockSpec((PEER_HEADS, i1_per_chunk, tile), lambda s, c: (0, c, s)),
            _const_spec((1, D_MODEL)),
            _const_spec((1, D_MODEL)),
        ],
        out_specs=pl.BlockSpec((tile, D_MODEL), lambda s, c: (s, 0)),
        out_shape=jax.ShapeDtypeStruct((n, D_MODEL), F32),
        scratch_shapes=[
            pltpu.VMEM((D_MODEL, tile), BF16),
            pltpu.VMEM((D_MODEL, tile), F32),
        ],
        compiler_params=pltpu.CompilerParams(
            dimension_semantics=("arbitrary", "arbitrary"),
            vmem_limit_bytes=V7X_VMEM_LIMIT_BYTES),
        name="peer_main",
    )(h, u_bf, vt_bf, rank2, e2, lrow, e1row, g, b)


def _peer_and_norm(h, wpq, sk, u_bf, vt_bf, g, b):
    rank2, e2, lrow, e1row = _peer_prep(h, wpq, sk)
    return _peer_main(h, u_bf, vt_bf, rank2, e2, lrow, e1row, g, b)


def kernel(x_prompt, x_sample, state_conv, cache_k, cache_v, w_in, conv_w, attn_sinks, w_conv_out, w_attn_out,
           w_o, ln1_g, ln1_b, w_pq, sub_keys, expert_u, expert_v, ln2_g, ln2_b):
    bsz, seq, _ = x_prompt.shape
    nb, t_new, _ = x_sample.shape
    wbuf = cache_k.shape[1]
    w_in_b = w_in.astype(BF16)
    wco = w_conv_out.astype(BF16)
    wao = w_attn_out.astype(BF16)
    wo = w_o.astype(BF16)
    wpq = w_pq.astype(BF16)
    sk = sub_keys.reshape(2 * PEER_HEADS, N_KEYS, D_HALF).astype(BF16)
    u_bf = expert_u.astype(BF16)
    vt_bf = expert_v.T.astype(BF16)
    sinks = attn_sinks.astype(F32)
    g1 = ln1_g.reshape(1, D_MODEL)
    b1 = ln1_b.reshape(1, D_MODEL)
    g2 = ln2_g.reshape(1, D_MODEL)
    b2 = ln2_b.reshape(1, D_MODEL)

    h_p, nk_p, nv_p, cs_p = _prompt_mixer(x_prompt, sinks, w_in_b, conv_w, wco, wao, wo, g1, b1)
    y_p = _peer_and_norm(h_p.reshape(bsz * seq, D_MODEL), wpq, sk, u_bf, vt_bf, g2, b2)

    h_s, cs_s, nk_s, nv_s = _sample_mixer(x_sample, state_conv, cache_k, cache_v, sinks, w_in_b, conv_w,
                                          wco, wao, wo, g1, b1)
    y_s = _peer_and_norm(h_s, wpq, sk, u_bf, vt_bf, g2, b2)

    return (y_p.reshape(bsz, seq, D_MODEL),
            y_s.reshape(nb, t_new, D_MODEL),
            cs_p,
            nk_p.reshape(bsz, WINDOW, N_KV_HEADS, HEAD_DIM),
            nv_p.reshape(bsz, WINDOW, N_KV_HEADS, HEAD_DIM),
            cs_s,
            nk_s.reshape(nb, wbuf, N_KV_HEADS, HEAD_DIM),
            nv_s.reshape(nb, wbuf, N_KV_HEADS, HEAD_DIM))
```
